```python
import math
import jax, jax.numpy as jnp
from jax import lax
import numpy as np

D_MODEL = 1024
BATCH = 4
SEQ = 4096
DEPTH = 1

SSM_D_INNER = D_MODEL
SSM_HEAD_DIM = 64
SSM_HEADS = SSM_D_INNER // SSM_HEAD_DIM
SSM_GROUPS = 2
SSM_STATE = 128
SSM_CHUNK = 128
LRU_WIDTH = D_MODEL
LRU_HEADS = 16
LRU_BLOCK = LRU_WIDTH // LRU_HEADS
LRU_C = 8.0
CONV_WIDTH = 4
D_FF = 4 * D_MODEL
NORM_EPS = 1e-6

SSM_XBC = SSM_D_INNER + 2 * SSM_GROUPS * SSM_STATE
IN_SPLIT_SIZES = [SSM_D_INNER, SSM_XBC, SSM_HEADS, LRU_WIDTH, LRU_WIDTH, D_MODEL, D_MODEL]
IN_SPLIT_IDX = [int(v) for v in np.cumsum(IN_SPLIT_SIZES)[:-1]]
D_IN_PROJ = int(sum(IN_SPLIT_SIZES))

kernel_name = "hybrid_ssd_rglru_gated_block"


def rms_norm(x, g):
    xf = x.astype(jnp.float32)
    var = jnp.mean(xf * xf, axis=-1, keepdims=True)
    return (xf * lax.rsqrt(var + NORM_EPS) * g.astype(jnp.float32)).astype(x.dtype)


def causal_depthwise_conv(x, w, b):
    k, c = w.shape
    out = lax.conv_general_dilated(
        x, w[:, None, :].astype(x.dtype), window_strides=(1,), padding=[(k - 1, 0)],
        dimension_numbers=("NWC", "WIO", "NWC"), feature_group_count=c)
    return out + b.astype(x.dtype)


def ssd_chunked(xs, dt, a, bm, cm):
    b, l, h, p = xs.shape
    g, n = bm.shape[2], bm.shape[3]
    r = h // g
    c = l // SSM_CHUNK
    L = SSM_CHUNK
    x = xs.reshape(b, c, L, g, r, p)
    dtc = dt.reshape(b, c, L, g, r)
    bc = bm.reshape(b, c, L, g, n)
    cc = cm.reshape(b, c, L, g, n)
    a_dt = dtc * a.reshape(g, r)
    a_cum = jnp.cumsum(a_dt, axis=2)
    seg = a_cum[:, :, :, None] - a_cum[:, :, None]
    mask = jnp.tril(jnp.ones((L, L), dtype=bool))[:, :, None, None]
    decay = jnp.exp(jnp.where(mask, seg, -jnp.inf))
    cb = jnp.einsum("bclgn,bcsgn->bclsg", cc, bc)
    wts = cb[..., None] * decay * dtc[:, :, None]
    y_diag = jnp.einsum("bclsgr,bcsgrp->bclgrp", wts, x)
    decay_states = jnp.exp(a_cum[:, :, -1:] - a_cum)
    states = jnp.einsum("bcsgn,bcsgr,bcsgrp->bcgrpn", bc, decay_states * dtc, x)
    chunk_decay = jnp.exp(a_cum[:, :, -1])

    def step(h_prev, inp):
        dec, st = inp
        h_new = dec[..., None, None] * h_prev + st
        return h_new, h_prev

    h0 = jnp.zeros((b, g, r, p, n), dtype=xs.dtype)
    _, prev = lax.scan(step, h0, (jnp.moveaxis(chunk_decay, 1, 0), jnp.moveaxis(states, 1, 0)))
    prev = jnp.moveaxis(prev, 0, 1)
    y_off = jnp.einsum("bclgn,bcgrpn,bclgr->bclgrp", cc, prev, jnp.exp(a_cum))
    return (y_diag + y_off).reshape(b, l, h, p)


def mamba2_branch(z, xbc, dt_raw, conv_w, conv_b, dt_bias, a_log, d_skip, norm_g):
    b, l, _ = z.shape
    xbc = jax.nn.silu(causal_depthwise_conv(xbc, conv_w, conv_b))
    xs, bm, cm = jnp.split(xbc, [SSM_D_INNER, SSM_D_INNER + SSM_GROUPS * SSM_STATE], axis=-1)
    xs = xs.reshape(b, l, SSM_HEADS, SSM_HEAD_DIM)
    bm = bm.reshape(b, l, SSM_GROUPS, SSM_STATE)
    cm = cm.reshape(b, l, SSM_GROUPS, SSM_STATE)
    dt = jax.nn.softplus(dt_raw + dt_bias)
    a = -jnp.exp(a_log)
    y = ssd_chunked(xs, dt, a, bm, cm) + d_skip[:, None] * xs
    y = y.reshape(b, l, SSM_D_INNER) * jax.nn.silu(z)
    yg = y.reshape(b, l, SSM_GROUPS, SSM_D_INNER // SSM_GROUPS)
    yg = yg * lax.rsqrt(jnp.mean(yg * yg, axis=-1, keepdims=True) + NORM_EPS)
    return yg.reshape(b, l, SSM_D_INNER) * norm_g


def rglru_branch(g_in, x_in, conv_w, conv_b, wa, ba, wx, bx, lam):
    b, l, _ = x_in.shape
    xr = causal_depthwise_conv(x_in, conv_w, conv_b)
    xb = xr.reshape(b, l, LRU_HEADS, LRU_BLOCK)
    gate_r = jax.nn.sigmoid(jnp.einsum("blhi,hij->blhj", xb, wa) + ba).reshape(b, l, LRU_WIDTH)
    gate_i = jax.nn.sigmoid(jnp.einsum("blhi,hij->blhj", xb, wx) + bx).reshape(b, l, LRU_WIDTH)
    log_a = -LRU_C * gate_r * jax.nn.softplus(-lam)
    a = jnp.exp(log_a)
    mult = jnp.sqrt(-jnp.expm1(2.0 * log_a))
    u = mult * (gate_i * xr)

    def combine(e1, e2):
        a1, b1 = e1
        a2, b2 = e2
        return a1 * a2, a2 * b1 + b2

    _, h = lax.associative_scan(combine, (a, u), axis=1)
    return h * jax.nn.gelu(g_in, approximate=True)


def setup_inputs(seed: int = 0) -> dict:
    key = jax.random.key(seed)
    ks = jax.random.split(key, 24)
    f32 = jnp.float32
    nrm = lambda k, shape, s: jax.random.normal(k, shape, f32) * s
    gain = lambda k, shape: 1.0 + 0.02 * jax.random.normal(k, shape, f32)
    dt0 = jnp.exp(jax.random.uniform(ks[5], (DEPTH, SSM_HEADS), f32, math.log(1e-3), math.log(1e-1)))
    a0 = jax.random.uniform(ks[15], (DEPTH, LRU_WIDTH), f32, 0.9, 0.999)
    s0 = a0 ** (1.0 / LRU_C)
    return {
        "x": jax.random.normal(ks[0], (BATCH, SEQ, D_MODEL), f32),
        "norm_mix_pre": gain(ks[1], (DEPTH, D_MODEL)),
        "w_in": nrm(ks[2], (DEPTH, D_MODEL, D_IN_PROJ), D_MODEL ** -0.5),
        "conv_ssm_w": nrm(ks[3], (DEPTH, CONV_WIDTH, SSM_XBC), CONV_WIDTH ** -0.5),
        "conv_ssm_b": nrm(ks[4], (DEPTH, SSM_XBC), 0.01),
        "dt_bias": dt0 + jnp.log(-jnp.expm1(-dt0)),
        "a_log": jnp.log(jax.random.uniform(ks[6], (DEPTH, SSM_HEADS), f32, 1.0, 16.0)),
        "d_skip": gain(ks[7], (DEPTH, SSM_HEADS)),
        "ssm_norm": gain(ks[8], (DEPTH, SSM_D_INNER)),
        "conv_lru_w": nrm(ks[9], (DEPTH, CONV_WIDTH, LRU_WIDTH), CONV_WIDTH ** -0.5),
        "conv_lru_b": nrm(ks[10], (DEPTH, LRU_WIDTH), 0.01),
        "lru_wa": nrm(ks[11], (DEPTH, LRU_HEADS, LRU_BLOCK, LRU_BLOCK), LRU_BLOCK ** -0.5),
        "lru_ba": nrm(ks[12], (DEPTH, LRU_WIDTH // LRU_BLOCK, LRU_BLOCK), 0.01),
        "lru_wx": nrm(ks[13], (DEPTH, LRU_HEADS, LRU_BLOCK, LRU_BLOCK), LRU_BLOCK ** -0.5),
        "lru_bx": nrm(ks[14], (DEPTH, LRU_WIDTH // LRU_BLOCK, LRU_BLOCK), 0.01),
        "lru_lambda": jnp.log(s0) - jnp.log1p(-s0),
        "w_out": nrm(ks[16], (DEPTH, D_MODEL, D_MODEL), D_MODEL ** -0.5),
        "norm_mix_post": gain(ks[17], (DEPTH, D_MODEL)),
        "norm_mlp_pre": gain(ks[18], (DEPTH, D_MODEL)),
        "w_up": nrm(ks[19], (DEPTH, D_MODEL, D_FF), D_MODEL ** -0.5),
        "w_down": nrm(ks[20], (DEPTH, D_FF, D_MODEL), D_FF ** -0.5),
        "norm_mlp_post": gain(ks[21], (DEPTH, D_MODEL)),
    }


def reference(x, norm_mix_pre, w_in, conv_ssm_w, conv_ssm_b, dt_bias, a_log, d_skip,
              ssm_norm, conv_lru_w, conv_lru_b, lru_wa, lru_ba, lru_wx, lru_bx,
              lru_lambda, w_out, norm_mix_post, norm_mlp_pre, w_up, w_down, norm_mlp_post):
    f32 = jnp.float32
    h = x
    for i in range(DEPTH):
        u = rms_norm(h, norm_mix_pre[i])
        proj = jnp.einsum("bsd,de->bse", u, w_in[i]).astype(f32)
        z, xbc, dt_raw, g_lru, x_lru, gate_a, gate_b = jnp.split(proj, IN_SPLIT_IDX, axis=-1)
        y_a = mamba2_branch(z, xbc, dt_raw, conv_ssm_w[i].astype(f32), conv_ssm_b[i].astype(f32),
                            dt_bias[i].astype(f32), a_log[i].astype(f32), d_skip[i].astype(f32),
                            ssm_norm[i].astype(f32))
        y_b = rglru_branch(g_lru, x_lru, conv_lru_w[i].astype(f32), conv_lru_b[i].astype(f32),
                           lru_wa[i].astype(f32), lru_ba[i].astype(f32),
                           lru_wx[i].astype(f32), lru_bx[i].astype(f32), lru_lambda[i].astype(f32))
        merged = (jax.nn.sigmoid(gate_a) * y_a + jax.nn.sigmoid(gate_b) * y_b).astype(h.dtype)
        mix = jnp.einsum("bsd,de->bse", merged, w_out[i])
        h = h + rms_norm(mix, norm_mix_post[i])
        v = rms_norm(h, norm_mlp_pre[i])
        hid = jnp.square(jax.nn.relu(jnp.einsum("bsd,df->bsf", v, w_up[i])))
        ff = jnp.einsum("bsf,fd->bsd", hid, w_down[i])
        h = h + rms_norm(ff, norm_mlp_post[i])
    return h
```

```python
import functools
import math

import jax
import jax.numpy as jnp
from jax import lax
from jax.experimental import pallas as pl
from jax.experimental.pallas import tpu as pltpu

NORM_EPS = 1e-6
LRU_C = 8.0
SSM_HEAD_DIM = 64
SSM_GROUPS = 2
SSM_STATE = 128
SSM_CHUNK = 128
LRU_BLOCK = 64
CONV_WIDTH = 4

V7X_LANES = 128
V7X_SUBLANES = 8
V7X_MXU_DIM = 256
VMEM_LIMIT_BYTES = 56 * 1024 * 1024

MIXER_TILE = 256
MLP_TILE = 512
MLP_FF_CHUNK = 1024

F32 = jnp.float32
BF16 = jnp.bfloat16


def _sigmoid(v):
    return 1.0 / (1.0 + jnp.exp(-v))


def _softplus(v):
    return jnp.maximum(v, 0.0) + jnp.log1p(jnp.exp(-jnp.abs(v)))


def _rms_scale(v, gain):
    ms = jnp.mean(v * v, axis=-1, keepdims=True)
    return v * lax.rsqrt(ms + NORM_EPS) * gain


def _dot(a, b):
    return jnp.dot(a, b, preferred_element_type=F32)


def _split_dot(v, rhs_bf16):
    hi = v.astype(BF16)
    lo = (v - hi.astype(F32)).astype(BF16)
    return _dot(hi, rhs_bf16) + _dot(lo, rhs_bf16)


def _causal_conv(buf_ref, w_ref, b_ref, rows):
    acc = b_ref[...] + w_ref[CONV_WIDTH - 1:CONV_WIDTH, :] * buf_ref[pl.ds(V7X_SUBLANES, rows), :]
    for back in range(1, CONV_WIDTH):
        k = CONV_WIDTH - 1 - back
        acc = acc + w_ref[k:k + 1, :] * buf_ref[pl.ds(V7X_SUBLANES - back, rows), :]
    return acc


def _linear_scan(a, b, h_prev, rows):
    row = lax.broadcasted_iota(jnp.int32, a.shape, 0)
    b = b + jnp.where(row == 0, a * h_prev, 0.0)
    width = a.shape[1]
    d = 1
    while d < rows:
        if d < V7X_SUBLANES:
            keep = row >= d
            a_sh = jnp.where(keep, pltpu.roll(a, d, 0), 1.0)
            b_sh = jnp.where(keep, pltpu.roll(b, d, 0), 0.0)
        else:
            a_sh = jnp.concatenate([jnp.ones((d, width), F32), a[:rows - d]], axis=0)
            b_sh = jnp.concatenate([jnp.zeros((d, width), F32), b[:rows - d]], axis=0)
        b = a * b_sh + b
        if 2 * d < rows:
            a = a * a_sh
        d *= 2
    return b


def _mixer_kernel(x_ref, gpre_ref, wmain_ref, wdt_ref, cw_ref, cb_ref, dtb_ref, alog_ref,
                  dskip_ref, snorm_ref, lcw_ref, lcb_ref, wa_ref, wx_ref, ba_ref, bx_ref,
                  lam_ref, wout_ref, gpost_ref, out_ref,
                  cbuf, lbuf, state, hlru, xs_s, b_s, c_s, dt_s, y_s,
                  *, tile, d_inner, n_heads):
    t = pl.program_id(1)
    xbc_w = d_inner + 2 * SSM_GROUPS * SSM_STATE
    gw = d_inner // SSM_GROUPS
    o_z, o_xbc = 0, d_inner
    o_g = o_xbc + xbc_w
    o_xl = o_g + d_inner
    o_ga = o_xl + d_inner
    o_gb = o_ga + d_inner

    @pl.when(t == 0)
    def _():
        cbuf[0:V7X_SUBLANES, :] = jnp.zeros((V7X_SUBLANES, xbc_w), F32)
        lbuf[0:V7X_SUBLANES, :] = jnp.zeros((V7X_SUBLANES, d_inner), F32)
        state[...] = jnp.zeros_like(state)
        hlru[...] = jnp.zeros_like(hlru)

    xt = x_ref[0]
    u = _rms_scale(xt, gpre_ref[...]).astype(BF16)

    cbuf[pl.ds(V7X_SUBLANES, tile), :] = _dot(u, wmain_ref[:, o_xbc:o_xbc + xbc_w])
    act = _causal_conv(cbuf, cw_ref, cb_ref, tile)
    cbuf[0:V7X_SUBLANES, :] = cbuf[pl.ds(tile, V7X_SUBLANES), :]
    act = act * _sigmoid(act)
    xs_s[...] = act[:, :d_inner]
    b_s[...] = act[:, d_inner:d_inner + SSM_GROUPS * SSM_STATE].astype(BF16)
    c_s[...] = act[:, d_inner + SSM_GROUPS * SSM_STATE:].astype(BF16)
    dt_s[...] = _softplus(_dot(u, wdt_ref[...]) + dtb_ref[...])

    a_row = -jnp.exp(alog_ref[...])
    ri = lax.broadcasted_iota(jnp.int32, (SSM_CHUNK, SSM_CHUNK), 0)
    ci = lax.broadcasted_iota(jnp.int32, (SSM_CHUNK, SSM_CHUNK), 1)
    causal = ri >= ci
    tri = causal.astype(F32)
    eh = lax.broadcasted_iota(jnp.int32, (V7X_LANES, d_inner), 0)
    ec = lax.broadcasted_iota(jnp.int32, (V7X_LANES, d_inner), 1)
    expand = ((ec >= eh * SSM_HEAD_DIM) & (ec < (eh + 1) * SSM_HEAD_DIM)).astype(BF16)
    lane = lax.broadcasted_iota(jnp.int32, (SSM_CHUNK, V7X_LANES), 1)
    low_half = lane < SSM_HEAD_DIM
    dskip_c = _split_dot(jnp.broadcast_to(dskip_ref[...], (V7X_SUBLANES, V7X_LANES)), expand)[0:1, :]

    def chunk_body(c, carry):
        r0 = pl.multiple_of(c * SSM_CHUNK, SSM_CHUNK)
        xs_c = xs_s[pl.ds(r0, SSM_CHUNK), :]
        xs_b = xs_c.astype(BF16)
        bm = b_s[pl.ds(r0, SSM_CHUNK), :]
        cm = c_s[pl.ds(r0, SSM_CHUNK), :]
        dt_c = dt_s[pl.ds(r0, SSM_CHUNK), :]
        a_cum = jnp.dot(tri, dt_c * a_row, precision=lax.Precision.HIGHEST,
                        preferred_element_type=F32)
        a_last = a_cum[SSM_CHUNK - 1:SSM_CHUNK, :]
        e_in = jnp.exp(a_cum)
        ds = dt_c * jnp.exp(a_last - a_cum)
        ex = _split_dot(jnp.concatenate([e_in, ds], axis=0), expand)
        e_in_c = ex[:SSM_CHUNK]
        xds = (xs_c * ex[SSM_CHUNK:]).astype(BF16)
        a_cum_t = a_cum.T
        dt_t = dt_c.T
        y_parts = []
        for g in range(SSM_GROUPS):
            bg = bm[:, g * SSM_STATE:(g + 1) * SSM_STATE]
            cg = cm[:, g * SSM_STATE:(g + 1) * SSM_STATE]
            cb = lax.dot_general(cg, bg, (((1,), (1,)), ((), ())), preferred_element_type=F32)
            s_prev = state[g]
            y_off = _dot(cg, s_prev.astype(BF16))
            s_new = lax.dot_general(bg, xds[:, g * gw:(g + 1) * gw], (((0,), (0,)), ((), ())),
                                    preferred_element_type=F32)
            state[g] = s_prev * e_in_c[SSM_CHUNK - 1:SSM_CHUNK, g * gw:(g + 1) * gw] + s_new
            heads_per_group = n_heads // SSM_GROUPS
            for pair in range(heads_per_group // 2):
                ws = []
                for h in (g * heads_per_group + 2 * pair, g * heads_per_group + 2 * pair + 1):
                    seg = a_cum[:, h:h + 1] - a_cum_t[h:h + 1, :]
                    dec = jnp.exp(jnp.where(causal, seg, -jnp.inf))
                    ws.append((cb * (dec * dt_t[h:h + 1, :])).astype(BF16))
                w_pair = jnp.concatenate(ws, axis=1)
                col = g * gw + pair * V7X_LANES
                xp = xs_b[:, col:col + V7X_LANES]
                zero = jnp.zeros_like(xp)
                x_bd = jnp.concatenate([jnp.where(low_half, xp, zero),
                                        jnp.where(low_half, zero, xp)], axis=0)
                y_d = _dot(w_pair, x_bd)
                lo = pair * V7X_LANES
                y_parts.append(y_d + y_off[:, lo:lo + V7X_LANES] * e_in_c[:, col:col + V7X_LANES])
        y_s[pl.ds(r0, SSM_CHUNK), :] = jnp.concatenate(y_parts, axis=1) + dskip_c * xs_c
        return carry

    lax.fori_loop(0, tile // SSM_CHUNK, chunk_body, 0)

    z = _dot(u, wmain_ref[:, o_z:o_z + d_inner])
    y = y_s[...] * (z * _sigmoid(z))
    y_a = jnp.concatenate(
        [_rms_scale(y[:, g * gw:(g + 1) * gw], 1.0) for g in range(SSM_GROUPS)], axis=1)
    y_a = y_a * snorm_ref[...]

    lbuf[pl.ds(V7X_SUBLANES, tile), :] = _dot(u, wmain_ref[:, o_xl:o_xl + d_inner])
    xr = _causal_conv(lbuf, lcw_ref, lcb_ref, tile)
    lbuf[0:V7X_SUBLANES, :] = lbuf[pl.ds(tile, V7X_SUBLANES), :]
    xr_b = xr.astype(BF16)
    n_bd = d_inner // V7X_MXU_DIM
    pre_r = jnp.concatenate(
        [_dot(xr_b[:, j * V7X_MXU_DIM:(j + 1) * V7X_MXU_DIM], wa_ref[j]) for j in range(n_bd)], axis=1)
    pre_i = jnp.concatenate(
        [_dot(xr_b[:, j * V7X_MXU_DIM:(j + 1) * V7X_MXU_DIM], wx_ref[j]) for j in range(n_bd)], axis=1)
    gate_r = _sigmoid(pre_r + ba_ref[...])
    gate_i = _sigmoid(pre_i + bx_ref[...])
    log_a = (-LRU_C * _softplus(-lam_ref[...])) * gate_r
    a = jnp.exp(log_a)
    mult = jnp.sqrt(-jnp.tanh(log_a) * (1.0 + a * a))
    h = _linear_scan(a, mult * (gate_i * xr), hlru[V7X_SUBLANES - 1:V7X_SUBLANES, :], tile)
    hlru[...] = h[tile - V7X_SUBLANES:, :]
    gl = _dot(u, wmain_ref[:, o_g:o_g + d_inner])
    gelu = 0.5 * gl * (1.0 + jnp.tanh(math.sqrt(2.0 / math.pi) * (gl + 0.044715 * (gl * gl * gl))))
    y_b = h * gelu

    merged = (_sigmoid(_dot(u, wmain_ref[:, o_ga:o_ga + d_inner])) * y_a
              + _sigmoid(_dot(u, wmain_ref[:, o_gb:o_gb + d_inner])) * y_b)
    mix = _dot(merged.astype(BF16), wout_ref[...])
    out_ref[0] = xt + _rms_scale(mix, gpost_ref[...])


def _mlp_kernel(h_ref, gpre_ref, wup_ref, wdown_ref, gpost_ref, out_ref, *, d_ff):
    h = h_ref[...]
    v = _rms_scale(h, gpre_ref[...]).astype(BF16)
    acc = jnp.zeros(h.shape, F32)
    for c in range(d_ff // MLP_FF_CHUNK):
        lo = c * MLP_FF_CHUNK
        hid = jnp.maximum(_dot(v, wup_ref[:, lo:lo + MLP_FF_CHUNK]), 0.0)
        acc = acc + _dot((hid * hid).astype(BF16), wdown_ref[lo:lo + MLP_FF_CHUNK, :])
    out_ref[...] = h + _rms_scale(acc, gpost_ref[...])


def _resident(shape):
    zeros = (0,) * len(shape)
    return pl.BlockSpec(shape, lambda *_: zeros, pipeline_mode=pl.Buffered(1))


def _block_diag(w):
    per = V7X_MXU_DIM // LRU_BLOCK
    heads = w.shape[0]
    w4 = w.reshape(heads // per, per, LRU_BLOCK, LRU_BLOCK)
    eye = jnp.eye(per, dtype=w.dtype)
    bd = w4[:, :, :, None, :] * eye[None, :, None, :, None]
    return bd.reshape(heads // per, V7X_MXU_DIM, V7X_MXU_DIM)


def _row(v, width=None):
    v = v.reshape(1, -1).astype(F32)
    if width is not None and v.shape[1] < width:
        v = jnp.pad(v, ((0, 0), (0, width - v.shape[1])))
    return v


def _mixer(x, p):
    batch, seq, d_model = x.shape
    d_inner = p["ssm_norm"].shape[0]
    n_heads = p["dt_bias"].shape[0]
    xbc_w = d_inner + 2 * SSM_GROUPS * SSM_STATE
    tile = MIXER_TILE
    assert seq % tile == 0 and tile % SSM_CHUNK == 0
    assert d_inner == n_heads * SSM_HEAD_DIM and n_heads <= V7X_LANES
    assert p["lru_lambda"].shape[0] == d_inner and d_model == d_inner

    w_in = p["w_in"]
    s = [d_inner, xbc_w, n_heads, d_inner, d_inner, d_model, d_model]
    off = [0]
    for v in s:
        off.append(off[-1] + v)
    assert w_in.shape[1] == off[-1]
    w_main = jnp.concatenate([w_in[:, off[0]:off[2]], w_in[:, off[3]:off[7]]], axis=1).astype(BF16)
    w_dt = jnp.pad(w_in[:, off[2]:off[3]], ((0, 0), (0, V7X_LANES - n_heads))).astype(BF16)

    operands = [
        x, _row(p["norm_mix_pre"]), w_main, w_dt,
        p["conv_ssm_w"].astype(F32), _row(p["conv_ssm_b"]),
        _row(p["dt_bias"], V7X_LANES), _row(p["a_log"], V7X_LANES), _row(p["d_skip"], V7X_LANES),
        _row(p["ssm_norm"]),
        p["conv_lru_w"].astype(F32), _row(p["conv_lru_b"]),
        _block_diag(p["lru_wa"]).astype(BF16), _block_diag(p["lru_wx"]).astype(BF16),
        _row(p["lru_ba"]), _row(p["lru_bx"]), _row(p["lru_lambda"]),
        p["w_out"].astype(BF16), _row(p["norm_mix_post"]),
    ]
    in_specs = [pl.BlockSpec((1, tile, d_model), lambda b, t: (b, t, 0))]
    in_specs += [_resident(o.shape) for o in operands[1:]]
    gw = d_inner // SSM_GROUPS
    scratch = [
        pltpu.VMEM((tile + V7X_SUBLANES, xbc_w), F32),
        pltpu.VMEM((tile + V7X_SUBLANES, d_inner), F32),
        pltpu.VMEM((SSM_GROUPS, SSM_STATE, gw), F32),
        pltpu.VMEM((V7X_SUBLANES, d_inner), F32),
        pltpu.VMEM((tile, d_inner), F32),
        pltpu.VMEM((tile, SSM_GROUPS * SSM_STATE), BF16),
        pltpu.VMEM((tile, SSM_GROUPS * SSM_STATE), BF16),
        pltpu.VMEM((tile, V7X_LANES), F32),
        pltpu.VMEM((tile, d_inner), F32),
    ]
    return pl.pallas_call(
        functools.partial(_mixer_kernel, tile=tile, d_inner=d_inner, n_heads=n_heads),
        out_shape=jax.ShapeDtypeStruct(x.shape, x.dtype),
        grid=(batch, seq // tile),
        in_specs=in_specs,
        out_specs=pl.BlockSpec((1, tile, d_model), lambda b, t: (b, t, 0)),
        scratch_shapes=scratch,
        compiler_params=pltpu.CompilerParams(
            dimension_semantics=("arbitrary", "arbitrary"), vmem_limit_bytes=VMEM_LIMIT_BYTES),
        name="mixer",
    )(*operands)


def _mlp(h, p):
    batch, seq, d_model = h.shape
    d_ff = p["w_up"].shape[1]
    rows = batch * seq
    assert rows % MLP_TILE == 0 and d_ff % MLP_FF_CHUNK == 0
    operands = [h.reshape(rows, d_model), _row(p["norm_mlp_pre"]), p["w_up"].astype(BF16),
                p["w_down"].astype(BF16), _row(p["norm_mlp_post"])]
    in_specs = [pl.BlockSpec((MLP_TILE, d_model), lambda i: (i, 0))]
    in_specs += [_resident(o.shape) for o in operands[1:]]
    out = pl.pallas_call(
        functools.partial(_mlp_kernel, d_ff=d_ff),
        out_shape=jax.ShapeDtypeStruct((rows, d_model), h.dtype),
        grid=(rows // MLP_TILE,),
        in_specs=in_specs,
        out_specs=pl.BlockSpec((MLP_TILE, d_model), lambda i: (i, 0)),
        compiler_params=pltpu.CompilerParams(
            dimension_semantics=("arbitrary",), vmem_limit_bytes=VMEM_LIMIT_BYTES),
        name="mlp",
    )(*operands)
    return out.reshape(batch, seq, d_model)


_PARAM_NAMES = ("norm_mix_pre", "w_in", "conv_ssm_w", "conv_ssm_b", "dt_bias", "a_log", "d_skip",
                "ssm_norm", "conv_lru_w", "conv_lru_b", "lru_wa", "lru_ba", "lru_wx", "lru_bx",
                "lru_lambda", "w_out", "norm_mix_post", "norm_mlp_pre", "w_up", "w_down",
                "norm_mlp_post")


def kernel(x, norm_mix_pre, w_in, conv_ssm_w, conv_ssm_b, dt_bias, a_log, d_skip, ssm_norm,
           conv_lru_w, conv_lru_b, lru_wa, lru_ba, lru_wx, lru_bx, lru_lambda, w_out,
           norm_mix_post, norm_mlp_pre, w_up, w_down, norm_mlp_post):
    stacked = dict(zip(_PARAM_NAMES, (
        norm_mix_pre, w_in, conv_ssm_w, conv_ssm_b, dt_bias, a_log, d_skip, ssm_norm, conv_lru_w,
        conv_lru_b, lru_wa, lru_ba, lru_wx, lru_bx, lru_lambda, w_out, norm_mix_post, norm_mlp_pre,
        w_up, w_down, norm_mlp_post)))
    h = x
    for layer in range(w_in.shape[0]):
        p = {k: v[layer] for k, v in stacked.items()}
        h = _mixer(h, p)
        h = _mlp(h, p)
    return h
```

```python
import functools
import math

import jax
import jax.numpy as jnp
from jax import lax
from jax.experimental import pallas as pl
from jax.experimental.pallas import tpu as pltpu

NORM_EPS = 1e-6
LRU_C = 8.0
SSM_HEAD_DIM = 64
SSM_GROUPS = 2
SSM_STATE = 128
SSM_CHUNK = 128
LRU_BLOCK = 64
CONV_WIDTH = 4
LOG2E = 1.4426950408889634

V7X_LANES = 128
V7X_SUBLANES = 8
V7X_MXU_DIM = 256
VMEM_LIMIT_BYTES = 56 * 1024 * 1024

SLABS = SSM_CHUNK // V7X_SUBLANES
MIXER_TILE = 256
MLP_TILE = 512
MLP_FF_CHUNK = 1024

F32 = jnp.float32
BF16 = jnp.bfloat16


def _sigmoid(v):
    return 1.0 / (1.0 + jnp.exp2(v * (-LOG2E)))


def _softplus(v):
    return jnp.maximum(v, 0.0) + jnp.log1p(jnp.exp(-jnp.abs(v)))


def _rms_scale(v, gain):
    ms = jnp.mean(v * v, axis=-1, keepdims=True)
    return v * lax.rsqrt(ms + NORM_EPS) * gain


def _dot(a, b):
    return jnp.dot(a, b, preferred_element_type=F32)


def _split_dot(v, rhs_bf16):
    hi = v.astype(BF16)
    lo = (v - hi.astype(F32)).astype(BF16)
    return _dot(hi, rhs_bf16) + _dot(lo, rhs_bf16)


def _row_time(r):
    return (r & (V7X_SUBLANES - 1)) * SLABS + (r >> 3)


def _perm_matrix(transpose):
    r = lax.broadcasted_iota(jnp.int32, (SSM_CHUNK, SSM_CHUNK), 0)
    c = lax.broadcasted_iota(jnp.int32, (SSM_CHUNK, SSM_CHUNK), 1)
    if transpose:
        r, c = c, r
    return (c == _row_time(r)).astype(BF16)


def _conv_slab_order(buf_ref, hist_ref, w_ref, b_ref, tile):
    width = buf_ref.shape[1]
    sub0 = lax.broadcasted_iota(jnp.int32, (V7X_SUBLANES, width), 0) == 0
    w = [w_ref[k:k + 1, :] for k in range(CONV_WIDTH)]
    bias = b_ref[...]
    outs = []
    for c in range(tile // SSM_CHUNK):
        base = c * SSM_CHUNK

        def rows(lo, hi, base=base):
            return buf_ref[base + V7X_SUBLANES * lo:base + V7X_SUBLANES * hi, :]

        back = []
        for i, j in enumerate(range(SLABS - CONV_WIDTH + 1, SLABS)):
            if c == 0:
                prev = hist_ref[V7X_SUBLANES * i:V7X_SUBLANES * (i + 1), :]
            else:
                prev = rows(j - SLABS, j + 1 - SLABS)
            back.append(jnp.where(sub0, pltpu.roll(prev, 1, 0), pltpu.roll(rows(j, j + 1), 1, 0)))
        r13, r14, r15 = back
        x0, x1, x2 = rows(0, 1), rows(1, 2), rows(2, 3)
        head = jnp.concatenate([
            bias + w[3] * x0 + w[2] * r15 + w[1] * r14 + w[0] * r13,
            bias + w[3] * x1 + w[2] * x0 + w[1] * r15 + w[0] * r14,
            bias + w[3] * x2 + w[2] * x1 + w[1] * x0 + w[0] * r15], axis=0)
        body = (bias + w[3] * rows(3, SLABS) + w[2] * rows(2, SLABS - 1)
                + w[1] * rows(1, SLABS - 2) + w[0] * rows(0, SLABS - 3))
        outs.append(jnp.concatenate([head, body], axis=0))
    hist_ref[...] = buf_ref[tile - (CONV_WIDTH - 1) * V7X_SUBLANES:tile, :]
    return outs


def _lru_scan_slab_order(a, b, carry_ref, h_ref, tile):
    width = a.shape[1]
    sub = lax.broadcasted_iota(jnp.int32, (V7X_SUBLANES, V7X_LANES), 0)
    for col in range(width // V7X_LANES):
        lanes = slice(col * V7X_LANES, (col + 1) * V7X_LANES)
        h0 = carry_ref[:, lanes]
        for c in range(tile // SSM_CHUNK):
            base = c * SSM_CHUNK
            hs, acs = [], []
            for j in range(SLABS):
                lo = base + V7X_SUBLANES * j
                aj, bj = a[lo:lo + V7X_SUBLANES, lanes], b[lo:lo + V7X_SUBLANES, lanes]
                hs.append(bj if j == 0 else aj * hs[-1] + bj)
                acs.append(aj if j == 0 else aj * acs[-1])
            pa, ph = acs[-1], hs[-1]
            d = 1
            while d < V7X_SUBLANES:
                keep = sub >= d
                ph = pa * jnp.where(keep, pltpu.roll(ph, d, 0), 0.0) + ph
                pa = pa * jnp.where(keep, pltpu.roll(pa, d, 0), 1.0)
                d *= 2
            seg_end = pa * h0 + ph
            h_in = jnp.where(sub == 0, h0, pltpu.roll(seg_end, 1, 0))
            for j in range(SLABS):
                lo = base + V7X_SUBLANES * j
                h_ref[lo:lo + V7X_SUBLANES, lanes] = hs[j] + acs[j] * h_in
            h0 = jnp.broadcast_to(seg_end[V7X_SUBLANES - 1:V7X_SUBLANES, :], (V7X_SUBLANES, V7X_LANES))
        carry_ref[:, lanes] = h0


def _mixer_kernel(x_ref, gpre_ref, wa_main_ref, wdt_ref, wb_main_ref, cw_ref, cb_ref, dtb_ref,
                  alog_ref, dskip_ref, snorm_ref, lcw_ref, lcb_ref, wa_ref, wx_ref, ba_ref, bx_ref,
                  lam_ref, wout_ref, gpost_ref, out_ref,
                  cbuf, chist, lbuf, lhist, state, hlru, xs_s, b_s, c_s, dt_s, y_s, h_s,
                  *, tile, d_inner, n_heads):
    t = pl.program_id(1)
    xbc_w = d_inner + 2 * SSM_GROUPS * SSM_STATE
    gw = d_inner // SSM_GROUPS
    n_chunks = tile // SSM_CHUNK
    o_z, o_xbc = 0, d_inner
    o_g, o_xl, o_ga, o_gb = 0, d_inner, 2 * d_inner, 3 * d_inner

    @pl.when(t == 0)
    def _():
        chist[...] = jnp.zeros_like(chist)
        lhist[...] = jnp.zeros_like(lhist)
        state[...] = jnp.zeros_like(state)
        hlru[...] = jnp.zeros_like(hlru)

    xt = x_ref[0]
    u_time = _rms_scale(xt, gpre_ref[...]).astype(BF16)
    to_slab = _perm_matrix(False)
    u = jnp.concatenate(
        [_dot(to_slab, u_time[c * SSM_CHUNK:(c + 1) * SSM_CHUNK]).astype(BF16)
         for c in range(n_chunks)], axis=0)

    cbuf[...] = _dot(u, wa_main_ref[:, o_xbc:o_xbc + xbc_w])
    for c, act in enumerate(_conv_slab_order(cbuf, chist, cw_ref, cb_ref, tile)):
        act = act * _sigmoid(act)
        rows = slice(c * SSM_CHUNK, (c + 1) * SSM_CHUNK)
        xs_s[rows, :] = act[:, :d_inner]
        b_s[rows, :] = act[:, d_inner:d_inner + SSM_GROUPS * SSM_STATE].astype(BF16)
        c_s[rows, :] = act[:, d_inner + SSM_GROUPS * SSM_STATE:].astype(BF16)
    dt_s[...] = _softplus(_dot(u, wdt_ref[...]) + dtb_ref[...])

    a_row = -jnp.exp(alog_ref[...])
    ri = lax.broadcasted_iota(jnp.int32, (SSM_CHUNK, SSM_CHUNK), 0)
    ci = lax.broadcasted_iota(jnp.int32, (SSM_CHUNK, SSM_CHUNK), 1)
    causal = _row_time(ri) >= _row_time(ci)
    tri = causal.astype(F32)
    eh = lax.broadcasted_iota(jnp.int32, (V7X_LANES, d_inner), 0)
    ec = lax.broadcasted_iota(jnp.int32, (V7X_LANES, d_inner), 1)
    expand = ((ec >= eh * SSM_HEAD_DIM) & (ec < (eh + 1) * SSM_HEAD_DIM)).astype(BF16)
    lane = lax.broadcasted_iota(jnp.int32, (SSM_CHUNK, V7X_LANES), 1)
    low_half = lane < SSM_HEAD_DIM
    dskip_c = _split_dot(jnp.broadcast_to(dskip_ref[...], (V7X_SUBLANES, V7X_LANES)), expand)[0:1, :]

    def chunk_body(c, carry):
        r0 = pl.multiple_of(c * SSM_CHUNK, SSM_CHUNK)
        xs_c = xs_s[pl.ds(r0, SSM_CHUNK), :]
        xs_b = xs_c.astype(BF16)
        bm = b_s[pl.ds(r0, SSM_CHUNK), :]
        cm = c_s[pl.ds(r0, SSM_CHUNK), :]
        dt_c = dt_s[pl.ds(r0, SSM_CHUNK), :]
        a_cum = jnp.dot(tri, dt_c * a_row, precision=lax.Precision.HIGHEST,
                        preferred_element_type=F32)
        a_last = a_cum[SSM_CHUNK - 1:SSM_CHUNK, :]
        e_in = jnp.exp(a_cum)
        ds = dt_c * jnp.exp(a_last - a_cum)
        ex = _split_dot(jnp.concatenate([e_in, ds], axis=0), expand)
        e_in_c = ex[:SSM_CHUNK]
        xds = (xs_c * ex[SSM_CHUNK:]).astype(BF16)
        a_cum_t = a_cum.T
        dt_t = dt_c.T
        y_parts = []
        for g in range(SSM_GROUPS):
            bg = bm[:, g * SSM_STATE:(g + 1) * SSM_STATE]
            cg = cm[:, g * SSM_STATE:(g + 1) * SSM_STATE]
            cb = lax.dot_general(cg, bg, (((1,), (1,)), ((), ())), preferred_element_type=F32)
            s_prev = state[g]
            y_off = _dot(cg, s_prev.astype(BF16))
            s_new = lax.dot_general(bg, xds[:, g * gw:(g + 1) * gw], (((0,), (0,)), ((), ())),
                                    preferred_element_type=F32)
            state[g] = s_prev * e_in_c[SSM_CHUNK - 1:SSM_CHUNK, g * gw:(g + 1) * gw] + s_new
            heads_per_group = n_heads // SSM_GROUPS
            for pair in range(heads_per_group // 2):
                ws = []
                for h in (g * heads_per_group + 2 * pair, g * heads_per_group + 2 * pair + 1):
                    seg = a_cum[:, h:h + 1] - a_cum_t[h:h + 1, :]
                    dec = jnp.exp(jnp.where(causal, seg, -jnp.inf))
                    ws.append((cb * (dec * dt_t[h:h + 1, :])).astype(BF16))
                w_pair = jnp.concatenate(ws, axis=1)
                col = g * gw + pair * V7X_LANES
                xp = xs_b[:, col:col + V7X_LANES]
                zero = jnp.zeros_like(xp)
                x_bd = jnp.concatenate([jnp.where(low_half, xp, zero),
                                        jnp.where(low_half, zero, xp)], axis=0)
                y_d = _dot(w_pair, x_bd)
                lo = pair * V7X_LANES
                y_parts.append(y_d + y_off[:, lo:lo + V7X_LANES] * e_in_c[:, col:col + V7X_LANES])
        y_s[pl.ds(r0, SSM_CHUNK), :] = jnp.concatenate(y_parts, axis=1) + dskip_c * xs_c
        return carry

    lax.fori_loop(0, n_chunks, chunk_body, 0)

    z = _dot(u, wa_main_ref[:, o_z:o_z + d_inner])
    y = y_s[...] * (z * _sigmoid(z))
    y_a = jnp.concatenate(
        [_rms_scale(y[:, g * gw:(g + 1) * gw], 1.0) for g in range(SSM_GROUPS)], axis=1)
    y_a = y_a * snorm_ref[...]

    lbuf[...] = _dot(u, wb_main_ref[:, o_xl:o_xl + d_inner])
    xr = jnp.concatenate(_conv_slab_order(lbuf, lhist, lcw_ref, lcb_ref, tile), axis=0)
    xr_b = xr.astype(BF16)
    n_bd = d_inner // V7X_MXU_DIM
    pre_r = jnp.concatenate(
        [_dot(xr_b[:, j * V7X_MXU_DIM:(j + 1) * V7X_MXU_DIM], wa_ref[j]) for j in range(n_bd)], axis=1)
    pre_i = jnp.concatenate(
        [_dot(xr_b[:, j * V7X_MXU_DIM:(j + 1) * V7X_MXU_DIM], wx_ref[j]) for j in range(n_bd)], axis=1)
    gate_r = _sigmoid(pre_r + ba_ref[...])
    gate_i = _sigmoid(pre_i + bx_ref[...])
    log_a = (-LRU_C * _softplus(-lam_ref[...])) * gate_r
    a = jnp.exp(log_a)
    mult = jnp.sqrt(-jnp.tanh(log_a) * (1.0 + a * a))
    _lru_scan_slab_order(a, mult * (gate_i * xr), hlru, h_s, tile)
    gl = _dot(u, wb_main_ref[:, o_g:o_g + d_inner])
    gelu = 0.5 * gl * (1.0 + jnp.tanh(math.sqrt(2.0 / math.pi) * (gl + 0.044715 * (gl * gl * gl))))
    y_b = h_s[...] * gelu

    merged = (_sigmoid(_dot(u, wb_main_ref[:, o_ga:o_ga + d_inner])) * y_a
              + _sigmoid(_dot(u, wb_main_ref[:, o_gb:o_gb + d_inner])) * y_b).astype(BF16)
    to_time = _perm_matrix(True)
    merged = jnp.concatenate(
        [_dot(to_time, merged[c * SSM_CHUNK:(c + 1) * SSM_CHUNK]).astype(BF16)
         for c in range(n_chunks)], axis=0)
    mix = _dot(merged, wout_ref[...])
    out_ref[0] = xt + _rms_scale(mix, gpost_ref[...])


def _mlp_kernel(h_ref, gpre_ref, wup_ref, wdown_ref, gpost_ref, out_ref, *, d_ff):
    h = h_ref[...]
    v = _rms_scale(h, gpre_ref[...]).astype(BF16)
    acc = jnp.zeros(h.shape, F32)
    for c in range(d_ff // MLP_FF_CHUNK):
        lo = c * MLP_FF_CHUNK
        hid = jnp.maximum(_dot(v, wup_ref[:, lo:lo + MLP_FF_CHUNK]), 0.0)
        acc = acc + _dot((hid * hid).astype(BF16), wdown_ref[lo:lo + MLP_FF_CHUNK, :])
    out_ref[...] = h + _rms_scale(acc, gpost_ref[...])


def _resident(shape):
    zeros = (0,) * len(shape)
    return pl.BlockSpec(shape, lambda *_: zeros, pipeline_mode=pl.Buffered(1))


def _block_diag(w):
    per = V7X_MXU_DIM // LRU_BLOCK
    heads = w.shape[0]
    w4 = w.reshape(heads // per, per, LRU_BLOCK, LRU_BLOCK)
    eye = jnp.eye(per, dtype=w.dtype)
    bd = w4[:, :, :, None, :] * eye[None, :, None, :, None]
    return bd.reshape(heads // per, V7X_MXU_DIM, V7X_MXU_DIM)


def _row(v, width=None):
    v = v.reshape(1, -1).astype(F32)
    if width is not None and v.shape[1] < width:
        v = jnp.pad(v, ((0, 0), (0, width - v.shape[1])))
    return v


def _mixer(x, p):
    batch, seq, d_model = x.shape
    d_inner = p["ssm_norm"].shape[0]
    n_heads = p["dt_bias"].shape[0]
    xbc_w = d_inner + 2 * SSM_GROUPS * SSM_STATE
    tile = MIXER_TILE
    assert seq % tile == 0 and tile % SSM_CHUNK == 0
    assert d_inner == n_heads * SSM_HEAD_DIM and n_heads <= V7X_LANES
    assert p["lru_lambda"].shape[0] == d_inner and d_model == d_inner

    w_in = p["w_in"]
    s = [d_inner, xbc_w, n_heads, d_inner, d_inner, d_model, d_model]
    off = [0]
    for v in s:
        off.append(off[-1] + v)
    assert w_in.shape[1] == off[-1]
    w_a = w_in[:, off[0]:off[2]].astype(BF16)
    w_dt = jnp.pad(w_in[:, off[2]:off[3]], ((0, 0), (0, V7X_LANES - n_heads))).astype(BF16)
    w_b = w_in[:, off[3]:off[7]].astype(BF16)

    operands = [
        x, _row(p["norm_mix_pre"]), w_a, w_dt, w_b,
        p["conv_ssm_w"].astype(F32), _row(p["conv_ssm_b"]),
        _row(p["dt_bias"], V7X_LANES), _row(p["a_log"], V7X_LANES), _row(p["d_skip"], V7X_LANES),
        _row(p["ssm_norm"]),
        p["conv_lru_w"].astype(F32), _row(p["conv_lru_b"]),
        _block_diag(p["lru_wa"]).astype(BF16), _block_diag(p["lru_wx"]).astype(BF16),
        _row(p["lru_ba"]), _row(p["lru_bx"]), _row(p["lru_lambda"]),
        p["w_out"].astype(BF16), _row(p["norm_mix_post"]),
    ]
    in_specs = [pl.BlockSpec((1, tile, d_model), lambda b, t: (b, t, 0))]
    in_specs += [_resident(o.shape) for o in operands[1:]]
    gw = d_inner // SSM_GROUPS
    scratch = [
        pltpu.VMEM((tile, xbc_w), F32),
        pltpu.VMEM(((CONV_WIDTH - 1) * V7X_SUBLANES, xbc_w), F32),
        pltpu.VMEM((tile, d_inner), F32),
        pltpu.VMEM(((CONV_WIDTH - 1) * V7X_SUBLANES, d_inner), F32),
        pltpu.VMEM((SSM_GROUPS, SSM_STATE, gw), F32),
        pltpu.VMEM((V7X_SUBLANES, d_inner), F32),
        pltpu.VMEM((tile, d_inner), F32),
        pltpu.VMEM((tile, SSM_GROUPS * SSM_STATE), BF16),
        pltpu.VMEM((tile, SSM_GROUPS * SSM_STATE), BF16),
        pltpu.VMEM((tile, V7X_LANES), F32),
        pltpu.VMEM((tile, d_inner), F32),
        pltpu.VMEM((tile, d_inner), F32),
    ]
    return pl.pallas_call(
        functools.partial(_mixer_kernel, tile=tile, d_inner=d_inner, n_heads=n_heads),
        out_shape=jax.ShapeDtypeStruct(x.shape, x.dtype),
        grid=(batch, seq // tile),
        in_specs=in_specs,
        out_specs=pl.BlockSpec((1, tile, d_model), lambda b, t: (b, t, 0)),
        scratch_shapes=scratch,
        compiler_params=pltpu.CompilerParams(
            dimension_semantics=("arbitrary", "arbitrary"), vmem_limit_bytes=VMEM_LIMIT_BYTES),
        name="mixer",
    )(*operands)


def _mlp(h, p):
    batch, seq, d_model = h.shape
    d_ff = p["w_up"].shape[1]
    rows = batch * seq
    assert rows % MLP_TILE == 0 and d_ff % MLP_FF_CHUNK == 0
    operands = [h.reshape(rows, d_model), _row(p["norm_mlp_pre"]), p["w_up"].astype(BF16),
                p["w_down"].astype(BF16), _row(p["norm_mlp_post"])]
    in_specs = [pl.BlockSpec((MLP_TILE, d_model), lambda i: (i, 0))]
    in_specs += [_resident(o.shape) for o in operands[1:]]
    out = pl.pallas_call(
        functools.partial(_mlp_kernel, d_ff=d_ff),
        out_shape=jax.ShapeDtypeStruct((rows, d_model), h.dtype),
        grid=(rows // MLP_TILE,),
        in_specs=in_specs,
        out_specs=pl.BlockSpec((MLP_TILE, d_model), lambda i: (i, 0)),
        compiler_params=pltpu.CompilerParams(
            dimension_semantics=("arbitrary",), vmem_limit_bytes=VMEM_LIMIT_BYTES),
        name="mlp",
    )(*operands)
    return out.reshape(batch, seq, d_model)


_PARAM_NAMES = ("norm_mix_pre", "w_in", "conv_ssm_w", "conv_ssm_b", "dt_bias", "a_log", "d_skip",
                "ssm_norm", "conv_lru_w", "conv_lru_b", "lru_wa", "lru_ba", "lru_wx", "lru_bx",
                "lru_lambda", "w_out", "norm_mix_post", "norm_mlp_pre", "w_up", "w_down",
                "norm_mlp_post")


def kernel(x, norm_mix_pre, w_in, conv_ssm_w, conv_ssm_b, dt_bias, a_log, d_skip, ssm_norm,
           conv_lru_w, conv_lru_b, lru_wa, lru_ba, lru_wx, lru_bx, lru_lambda, w_out,
           norm_mix_post, norm_mlp_pre, w_up, w_down, norm_mlp_post):
    stacked = dict(zip(_PARAM_NAMES, (
        norm_mix_pre, w_in, conv_ssm_w, conv_ssm_b, dt_bias, a_log, d_skip, ssm_norm, conv_lru_w,
        conv_lru_b, lru_wa, lru_ba, lru_wx, lru_bx, lru_lambda, w_out, norm_mix_post, norm_mlp_pre,
        w_up, w_down, norm_mlp_post)))
    h = x
    for layer in range(w_in.shape[0]):
        p = {k: v[layer] for k, v in stacked.items()}
        h = _mixer(h, p)
        h = _mlp(h, p)
    return h
```

```python
import functools
import math

import jax
import jax.numpy as jnp
from jax import lax
from jax.experimental import pallas as pl
from jax.experimental.pallas import tpu as pltpu

NORM_EPS = 1e-6
LRU_C = 8.0
SSM_HEAD_DIM = 64
SSM_GROUPS = 2
SSM_STATE = 128
SSM_CHUNK = 128
LRU_BLOCK = 64
CONV_WIDTH = 4
LOG2E = 1.4426950408889634

V7X_LANES = 128
V7X_SUBLANES = 8
V7X_MXU_DIM = 256
VMEM_LIMIT_BYTES = 56 * 1024 * 1024

SLABS = SSM_CHUNK // V7X_SUBLANES
MIXER_TILE = 512
MLP_TILE = 512
MLP_FF_CHUNK = 1024

F32 = jnp.float32
BF16 = jnp.bfloat16


def _sigmoid(v):
    return 1.0 / (1.0 + jnp.exp2(v * (-LOG2E)))


def _softplus(v):
    return jnp.maximum(v, 0.0) + jnp.log1p(jnp.exp(-jnp.abs(v)))


def _rms_scale(v, gain):
    ms = jnp.mean(v * v, axis=-1, keepdims=True)
    return v * lax.rsqrt(ms + NORM_EPS) * gain


def _dot(a, b):
    return jnp.dot(a, b, preferred_element_type=F32)


def _split_dot(v, rhs_bf16):
    hi = v.astype(BF16)
    lo = (v - hi.astype(F32)).astype(BF16)
    return _dot(hi, rhs_bf16) + _dot(lo, rhs_bf16)


def _row_time(r):
    return (r & (V7X_SUBLANES - 1)) * SLABS + (r >> 3)


def _perm_matrix(transpose):
    r = lax.broadcasted_iota(jnp.int32, (SSM_CHUNK, SSM_CHUNK), 0)
    c = lax.broadcasted_iota(jnp.int32, (SSM_CHUNK, SSM_CHUNK), 1)
    if transpose:
        r, c = c, r
    return (c == _row_time(r)).astype(BF16)


def _conv_slab_order(buf_ref, hist_ref, w_ref, b_ref, tile):
    width = buf_ref.shape[1]
    sub0 = lax.broadcasted_iota(jnp.int32, (V7X_SUBLANES, width), 0) == 0
    w = [w_ref[k:k + 1, :] for k in range(CONV_WIDTH)]
    bias = b_ref[...]
    outs = []
    for c in range(tile // SSM_CHUNK):
        base = c * SSM_CHUNK

        def rows(lo, hi, base=base):
            return buf_ref[base + V7X_SUBLANES * lo:base + V7X_SUBLANES * hi, :]

        back = []
        for i, j in enumerate(range(SLABS - CONV_WIDTH + 1, SLABS)):
            if c == 0:
                prev = hist_ref[V7X_SUBLANES * i:V7X_SUBLANES * (i + 1), :]
            else:
                prev = rows(j - SLABS, j + 1 - SLABS)
            back.append(jnp.where(sub0, pltpu.roll(prev, 1, 0), pltpu.roll(rows(j, j + 1), 1, 0)))
        r13, r14, r15 = back
        x0, x1, x2 = rows(0, 1), rows(1, 2), rows(2, 3)
        head = jnp.concatenate([
            bias + w[3] * x0 + w[2] * r15 + w[1] * r14 + w[0] * r13,
            bias + w[3] * x1 + w[2] * x0 + w[1] * r15 + w[0] * r14,
            bias + w[3] * x2 + w[2] * x1 + w[1] * x0 + w[0] * r15], axis=0)
        body = (bias + w[3] * rows(3, SLABS) + w[2] * rows(2, SLABS - 1)
                + w[1] * rows(1, SLABS - 2) + w[0] * rows(0, SLABS - 3))
        outs.append(jnp.concatenate([head, body], axis=0))
    hist_ref[...] = buf_ref[tile - (CONV_WIDTH - 1) * V7X_SUBLANES:tile, :]
    return outs


def _lru_scan_slab_order(a, b, carry_ref, h_ref, tile):
    width = a.shape[1]
    sub = lax.broadcasted_iota(jnp.int32, (V7X_SUBLANES, V7X_LANES), 0)
    for col in range(width // V7X_LANES):
        lanes = slice(col * V7X_LANES, (col + 1) * V7X_LANES)
        h0 = carry_ref[:, lanes]
        for c in range(tile // SSM_CHUNK):
            base = c * SSM_CHUNK
            hs, acs = [], []
            for j in range(SLABS):
                lo = base + V7X_SUBLANES * j
                aj, bj = a[lo:lo + V7X_SUBLANES, lanes], b[lo:lo + V7X_SUBLANES, lanes]
                hs.append(bj if j == 0 else aj * hs[-1] + bj)
                acs.append(aj if j == 0 else aj * acs[-1])
            pa, ph = acs[-1], hs[-1]
            d = 1
            while d < V7X_SUBLANES:
                keep = sub >= d
                ph = pa * jnp.where(keep, pltpu.roll(ph, d, 0), 0.0) + ph
                pa = pa * jnp.where(keep, pltpu.roll(pa, d, 0), 1.0)
                d *= 2
            seg_end = pa * h0 + ph
            h_in = jnp.where(sub == 0, h0, pltpu.roll(seg_end, 1, 0))
            for j in range(SLABS):
                lo = base + V7X_SUBLANES * j
                h_ref[lo:lo + V7X_SUBLANES, lanes] = hs[j] + acs[j] * h_in
            h0 = jnp.broadcast_to(seg_end[V7X_SUBLANES - 1:V7X_SUBLANES, :], (V7X_SUBLANES, V7X_LANES))
        carry_ref[:, lanes] = h0


def _mixer_kernel(x_ref, gpre_ref, wa_main_ref, wdt_ref, wb_main_ref, cw_ref, cb_ref, dtb_ref,
                  alog_ref, dskip_ref, snorm_ref, lcw_ref, lcb_ref, wa_ref, wx_ref, ba_ref, bx_ref,
                  lam_ref, wout_ref, gpost_ref, out_ref,
                  cbuf, chist, lbuf, lhist, state, hlru, xs_s, b_s, c_s, dt_s, y_s, h_s,
                  *, tile, d_inner, n_heads):
    t = pl.program_id(1)
    xbc_w = d_inner + 2 * SSM_GROUPS * SSM_STATE
    gw = d_inner // SSM_GROUPS
    n_chunks = tile // SSM_CHUNK
    o_z, o_xbc = 0, d_inner
    o_g, o_xl, o_ga, o_gb = 0, d_inner, 2 * d_inner, 3 * d_inner

    @pl.when(t == 0)
    def _():
        chist[...] = jnp.zeros_like(chist)
        lhist[...] = jnp.zeros_like(lhist)
        state[...] = jnp.zeros_like(state)
        hlru[...] = jnp.zeros_like(hlru)

    xt = x_ref[0]
    u_time = _rms_scale(xt, gpre_ref[...]).astype(BF16)
    to_slab = _perm_matrix(False)
    u = jnp.concatenate(
        [_dot(to_slab, u_time[c * SSM_CHUNK:(c + 1) * SSM_CHUNK]).astype(BF16)
         for c in range(n_chunks)], axis=0)

    cbuf[...] = _dot(u, wa_main_ref[:, o_xbc:o_xbc + xbc_w])
    for c, act in enumerate(_conv_slab_order(cbuf, chist, cw_ref, cb_ref, tile)):
        act = act * _sigmoid(act)
        rows = slice(c * SSM_CHUNK, (c + 1) * SSM_CHUNK)
        xs_s[rows, :] = act[:, :d_inner]
        b_s[rows, :] = act[:, d_inner:d_inner + SSM_GROUPS * SSM_STATE].astype(BF16)
        c_s[rows, :] = act[:, d_inner + SSM_GROUPS * SSM_STATE:].astype(BF16)
    dt_s[...] = _softplus(_dot(u, wdt_ref[...]) + dtb_ref[...])

    a_row = -jnp.exp(alog_ref[...])
    ri = lax.broadcasted_iota(jnp.int32, (SSM_CHUNK, SSM_CHUNK), 0)
    ci = lax.broadcasted_iota(jnp.int32, (SSM_CHUNK, SSM_CHUNK), 1)
    causal = _row_time(ri) >= _row_time(ci)
    tri = causal.astype(F32)
    eh = lax.broadcasted_iota(jnp.int32, (V7X_LANES, d_inner), 0)
    ec = lax.broadcasted_iota(jnp.int32, (V7X_LANES, d_inner), 1)
    expand = ((ec >= eh * SSM_HEAD_DIM) & (ec < (eh + 1) * SSM_HEAD_DIM)).astype(BF16)
    lane = lax.broadcasted_iota(jnp.int32, (SSM_CHUNK, V7X_LANES), 1)
    low_half = lane < SSM_HEAD_DIM
    dskip_c = _split_dot(jnp.broadcast_to(dskip_ref[...], (V7X_SUBLANES, V7X_LANES)), expand)[0:1, :]

    def chunk_body(c):
        r0 = c * SSM_CHUNK
        xs_c = xs_s[pl.ds(r0, SSM_CHUNK), :]
        xs_b = xs_c.astype(BF16)
        bm = b_s[pl.ds(r0, SSM_CHUNK), :]
        cm = c_s[pl.ds(r0, SSM_CHUNK), :]
        dt_c = dt_s[pl.ds(r0, SSM_CHUNK), :]
        a_cum = jnp.dot(tri, dt_c * a_row, precision=lax.Precision.HIGHEST,
                        preferred_element_type=F32)
        a_last = a_cum[SSM_CHUNK - 1:SSM_CHUNK, :]
        e_in = jnp.exp(a_cum)
        ds = dt_c * jnp.exp(a_last - a_cum)
        ex = _split_dot(jnp.concatenate([e_in, ds], axis=0), expand)
        e_in_c = ex[:SSM_CHUNK]
        xds = (xs_c * ex[SSM_CHUNK:]).astype(BF16)
        a_cum_t = a_cum.T
        dt_t = dt_c.T
        y_parts = []
        for g in range(SSM_GROUPS):
            bg = bm[:, g * SSM_STATE:(g + 1) * SSM_STATE]
            cg = cm[:, g * SSM_STATE:(g + 1) * SSM_STATE]
            cb = lax.dot_general(cg, bg, (((1,), (1,)), ((), ())), preferred_element_type=F32)
            s_prev = state[g]
            y_off = _dot(cg, s_prev.astype(BF16))
            s_new = lax.dot_general(bg, xds[:, g * gw:(g + 1) * gw], (((0,), (0,)), ((), ())),
                                    preferred_element_type=F32)
            state[g] = s_prev * e_in_c[SSM_CHUNK - 1:SSM_CHUNK, g * gw:(g + 1) * gw] + s_new
            heads_per_group = n_heads // SSM_GROUPS
            for pair in range(heads_per_group // 2):
                ws = []
                for h in (g * heads_per_group + 2 * pair, g * heads_per_group + 2 * pair + 1):
                    seg = a_cum[:, h:h + 1] - a_cum_t[h:h + 1, :]
                    dec = jnp.exp(jnp.where(causal, seg, -jnp.inf))
                    ws.append((cb * (dec * dt_t[h:h + 1, :])).astype(BF16))
                w_pair = jnp.concatenate(ws, axis=1)
                col = g * gw + pair * V7X_LANES
                xp = xs_b[:, col:col + V7X_LANES]
                zero = jnp.zeros_like(xp)
                x_bd = jnp.concatenate([jnp.where(low_half, xp, zero),
                                        jnp.where(low_half, zero, xp)], axis=0)
                y_d = _dot(w_pair, x_bd)
                lo = pair * V7X_LANES
                y_parts.append(y_d + y_off[:, lo:lo + V7X_LANES] * e_in_c[:, col:col + V7X_LANES])
        y_s[pl.ds(r0, SSM_CHUNK), :] = jnp.concatenate(y_parts, axis=1) + dskip_c * xs_c

    for c in range(n_chunks):
        chunk_body(c)

    z = _dot(u, wa_main_ref[:, o_z:o_z + d_inner])
    y = y_s[...] * (z * _sigmoid(z))
    y_a = jnp.concatenate(
        [_rms_scale(y[:, g * gw:(g + 1) * gw], 1.0) for g in range(SSM_GROUPS)], axis=1)
    y_a = y_a * snorm_ref[...]

    lbuf[...] = _dot(u, wb_main_ref[:, o_xl:o_xl + d_inner])
    xr = jnp.concatenate(_conv_slab_order(lbuf, lhist, lcw_ref, lcb_ref, tile), axis=0)
    xr_b = xr.astype(BF16)
    n_bd = d_inner // V7X_MXU_DIM
    pre_r = jnp.concatenate(
        [_dot(xr_b[:, j * V7X_MXU_DIM:(j + 1) * V7X_MXU_DIM], wa_ref[j]) for j in range(n_bd)], axis=1)
    pre_i = jnp.concatenate(
        [_dot(xr_b[:, j * V7X_MXU_DIM:(j + 1) * V7X_MXU_DIM], wx_ref[j]) for j in range(n_bd)], axis=1)
    gate_r = _sigmoid(pre_r + ba_ref[...])
    gate_i = _sigmoid(pre_i + bx_ref[...])
    log_a = (-LRU_C * _softplus(-lam_ref[...])) * gate_r
    a = jnp.exp(log_a)
    m2 = -jnp.tanh(log_a) * (1.0 + a * a)
    mult = jnp.where(m2 > 0.0, m2 * lax.rsqrt(m2), 0.0)
    _lru_scan_slab_order(a, mult * (gate_i * xr), hlru, h_s, tile)
    gl = _dot(u, wb_main_ref[:, o_g:o_g + d_inner])
    gelu = 0.5 * gl * (1.0 + jnp.tanh(math.sqrt(2.0 / math.pi) * (gl + 0.044715 * (gl * gl * gl))))
    y_b = h_s[...] * gelu

    merged = (_sigmoid(_dot(u, wb_main_ref[:, o_ga:o_ga + d_inner])) * y_a
              + _sigmoid(_dot(u, wb_main_ref[:, o_gb:o_gb + d_inner])) * y_b).astype(BF16)
    to_time = _perm_matrix(True)
    merged = jnp.concatenate(
        [_dot(to_time, merged[c * SSM_CHUNK:(c + 1) * SSM_CHUNK]).astype(BF16)
         for c in range(n_chunks)], axis=0)
    mix = _dot(merged, wout_ref[...])
    out_ref[0] = xt + _rms_scale(mix, gpost_ref[...])


def _mlp_kernel(h_ref, gpre_ref, wup_ref, wdown_ref, gpost_ref, out_ref, *, d_ff):
    h = h_ref[...]
    v = _rms_scale(h, gpre_ref[...]).astype(BF16)
    acc = jnp.zeros(h.shape, F32)
    for c in range(d_ff // MLP_FF_CHUNK):
        lo = c * MLP_FF_CHUNK
        hid = jnp.maximum(_dot(v, wup_ref[:, lo:lo + MLP_FF_CHUNK]), 0.0)
        acc = acc + _dot((hid * hid).astype(BF16), wdown_ref[lo:lo + MLP_FF_CHUNK, :])
    out_ref[...] = h + _rms_scale(acc, gpost_ref[...])


def _resident(shape):
    zeros = (0,) * len(shape)
    return pl.BlockSpec(shape, lambda *_: zeros, pipeline_mode=pl.Buffered(1))


def _block_diag(w):
    per = V7X_MXU_DIM // LRU_BLOCK
    heads = w.shape[0]
    w4 = w.reshape(heads // per, per, LRU_BLOCK, LRU_BLOCK)
    eye = jnp.eye(per, dtype=w.dtype)
    bd = w4[:, :, :, None, :] * eye[None, :, None, :, None]
    return bd.reshape(heads // per, V7X_MXU_DIM, V7X_MXU_DIM)


def _row(v, width=None):
    v = v.reshape(1, -1).astype(F32)
    if width is not None and v.shape[1] < width:
        v = jnp.pad(v, ((0, 0), (0, width - v.shape[1])))
    return v


def _mixer(x, p):
    batch, seq, d_model = x.shape
    d_inner = p["ssm_norm"].shape[0]
    n_heads = p["dt_bias"].shape[0]
    xbc_w = d_inner + 2 * SSM_GROUPS * SSM_STATE
    tile = MIXER_TILE
    assert seq % tile == 0 and tile % SSM_CHUNK == 0
    assert d_inner == n_heads * SSM_HEAD_DIM and n_heads <= V7X_LANES
    assert p["lru_lambda"].shape[0] == d_inner and d_model == d_inner

    w_in = p["w_in"]
    s = [d_inner, xbc_w, n_heads, d_inner, d_inner, d_model, d_model]
    off = [0]
    for v in s:
        off.append(off[-1] + v)
    assert w_in.shape[1] == off[-1]
    w_a = w_in[:, off[0]:off[2]].astype(BF16)
    w_dt = jnp.pad(w_in[:, off[2]:off[3]], ((0, 0), (0, V7X_LANES - n_heads))).astype(BF16)
    w_b = w_in[:, off[3]:off[7]].astype(BF16)

    operands = [
        x, _row(p["norm_mix_pre"]), w_a, w_dt, w_b,
        p["conv_ssm_w"].astype(F32), _row(p["conv_ssm_b"]),
        _row(p["dt_bias"], V7X_LANES), _row(p["a_log"], V7X_LANES), _row(p["d_skip"], V7X_LANES),
        _row(p["ssm_norm"]),
        p["conv_lru_w"].astype(F32), _row(p["conv_lru_b"]),
        _block_diag(p["lru_wa"]).astype(BF16), _block_diag(p["lru_wx"]).astype(BF16),
        _row(p["lru_ba"]), _row(p["lru_bx"]), _row(p["lru_lambda"]),
        p["w_out"].astype(BF16), _row(p["norm_mix_post"]),
    ]
    in_specs = [pl.BlockSpec((1, tile, d_model), lambda b, t: (b, t, 0))]
    in_specs += [_resident(o.shape) for o in operands[1:]]
    gw = d_inner // SSM_GROUPS
    scratch = [
        pltpu.VMEM((tile, xbc_w), F32),
        pltpu.VMEM(((CONV_WIDTH - 1) * V7X_SUBLANES, xbc_w), F32),
        pltpu.VMEM((tile, d_inner), F32),
        pltpu.VMEM(((CONV_WIDTH - 1) * V7X_SUBLANES, d_inner), F32),
        pltpu.VMEM((SSM_GROUPS, SSM_STATE, gw), F32),
        pltpu.VMEM((V7X_SUBLANES, d_inner), F32),
        pltpu.VMEM((tile, d_inner), F32),
        pltpu.VMEM((tile, SSM_GROUPS * SSM_STATE), BF16),
        pltpu.VMEM((tile, SSM_GROUPS * SSM_STATE), BF16),
        pltpu.VMEM((tile, V7X_LANES), F32),
        pltpu.VMEM((tile, d_inner), F32),
        pltpu.VMEM((tile, d_inner), F32),
    ]
    return pl.pallas_call(
        functools.partial(_mixer_kernel, tile=tile, d_inner=d_inner, n_heads=n_heads),
        out_shape=jax.ShapeDtypeStruct(x.shape, x.dtype),
        grid=(batch, seq // tile),
        in_specs=in_specs,
        out_specs=pl.BlockSpec((1, tile, d_model), lambda b, t: (b, t, 0)),
        scratch_shapes=scratch,
        compiler_params=pltpu.CompilerParams(
            dimension_semantics=("arbitrary", "arbitrary"), vmem_limit_bytes=VMEM_LIMIT_BYTES),
        name="mixer",
    )(*operands)


def _mlp(h, p):
    batch, seq, d_model = h.shape
    d_ff = p["w_up"].shape[1]
    rows = batch * seq
    assert rows % MLP_TILE == 0 and d_ff % MLP_FF_CHUNK == 0
    operands = [h.reshape(rows, d_model), _row(p["norm_mlp_pre"]), p["w_up"].astype(BF16),
                p["w_down"].astype(BF16), _row(p["norm_mlp_post"])]
    in_specs = [pl.BlockSpec((MLP_TILE, d_model), lambda i: (i, 0))]
    in_specs += [_resident(o.shape) for o in operands[1:]]
    out = pl.pallas_call(
        functools.partial(_mlp_kernel, d_ff=d_ff),
        out_shape=jax.ShapeDtypeStruct((rows, d_model), h.dtype),
        grid=(rows // MLP_TILE,),
        in_specs=in_specs,
        out_specs=pl.BlockSpec((MLP_TILE, d_model), lambda i: (i, 0)),
        compiler_params=pltpu.CompilerParams(
            dimension_semantics=("arbitrary",), vmem_limit_bytes=VMEM_LIMIT_BYTES),
        name="mlp",
    )(*operands)
    return out.reshape(batch, seq, d_model)


_PARAM_NAMES = ("norm_mix_pre", "w_in", "conv_ssm_w", "conv_ssm_b", "dt_bias", "a_log", "d_skip",
                "ssm_norm", "conv_lru_w", "conv_lru_b", "lru_wa", "lru_ba", "lru_wx", "lru_bx",
                "lru_lambda", "w_out", "norm_mix_post", "norm_mlp_pre", "w_up", "w_down",
                "norm_mlp_post")


def kernel(x, norm_mix_pre, w_in, conv_ssm_w, conv_ssm_b, dt_bias, a_log, d_skip, ssm_norm,
           conv_lru_w, conv_lru_b, lru_wa, lru_ba, lru_wx, lru_bx, lru_lambda, w_out,
           norm_mix_post, norm_mlp_pre, w_up, w_down, norm_mlp_post):
    stacked = dict(zip(_PARAM_NAMES, (
        norm_mix_pre, w_in, conv_ssm_w, conv_ssm_b, dt_bias, a_log, d_skip, ssm_norm, conv_lru_w,
        conv_lru_b, lru_wa, lru_ba, lru_wx, lru_bx, lru_lambda, w_out, norm_mix_post, norm_mlp_pre,
        w_up, w_down, norm_mlp_post)))
    h = x
    for layer in range(w_in.shape[0]):
        p = {k: v[layer] for k, v in stacked.items()}
        h = _mixer(h, p)
        h = _mlp(h, p)
    return h
```
